```python
import jax, jax.numpy as jnp
from jax import lax
import numpy as np

D_MODEL = 1024
BATCH = 32
SEQ = 2048
DEPTH = 4
DEC_BATCH = 8
DEC_SEQ = 32
PAST_LEN = 2048

CHUNK = 64
N_MIXERS = 2
EXPAND = 2
E_A = EXPAND * D_MODEL
E_B = EXPAND * D_MODEL
CONV_W = 3
POOL_WINDOWS = (2, 4, 8, 16)
N_POOL_GROUPS = len(POOL_WINDOWS)
G_B = E_B // N_POOL_GROUPS
POOL_HIST = max(POOL_WINDOWS) - 1
P_DIM = 256
N_A = (DEPTH + 1) // 2
N_B = DEPTH // 2
EPS = 1e-6

kernel_name = "hybrid_conv_pool_stream_step"


def rmsnorm(x, g):
    xf = x.astype(jnp.float32)
    r = lax.rsqrt(jnp.mean(xf * xf, axis=-1, keepdims=True) + EPS)
    return (xf * r).astype(x.dtype) * g


def conv_mixer(hn, hist, w_in, conv_w, w_out):
    proj = hn @ w_in
    b_gate, c_gate, h, z = jnp.split(proj, 4, axis=-1)
    v = c_gate * h
    L = v.shape[1]
    vp = jnp.concatenate([hist.astype(v.dtype), v], axis=1)
    y = conv_w[0] * vp[:, 0:L]
    for k in range(1, CONV_W):
        y = y + conv_w[k] * vp[:, k:k + L]
    y = b_gate * y * jax.nn.silu(z)
    return y @ w_out, vp[:, -(CONV_W - 1):]


def pool_mixer(hn, hist, pos0, w_in, w_grp, scale, w_out):
    proj = hn @ w_in
    u, z = jnp.split(proj, 2, axis=-1)
    Bsz, L, E = u.shape
    up = jnp.concatenate([hist.astype(u.dtype), u], axis=1)
    cs = jnp.cumsum(up.astype(jnp.float32), axis=1)
    cs = jnp.concatenate([jnp.zeros((Bsz, 1, E), jnp.float32), cs], axis=1)
    pos = pos0 + jnp.arange(L)
    uf = u.astype(jnp.float32)
    outs = []
    for j, w in enumerate(POOL_WINDOWS):
        sl = slice(j * G_B, (j + 1) * G_B)
        s = cs[:, POOL_HIST + 1:, sl] - cs[:, POOL_HIST + 1 - w:POOL_HIST + 1 - w + L, sl]
        cnt = jnp.minimum(pos + 1, w).astype(jnp.float32)
        outs.append(s / cnt[None, :, None] - uf[..., sl])
    d = jnp.stack(outs, axis=2).astype(u.dtype)
    mixed = jnp.einsum('blgc,gcd->blgd', d, w_grp).reshape(Bsz, L, E) * scale
    y = mixed * jax.nn.silu(z)
    return y @ w_out, up[:, -POOL_HIST:]


def trunk(x, p, conv_hist, pool_hist, pos0, norm_g, w_in_a, conv_w_a, w_out_a,
          w_in_b, w_grp_b, scale_b, w_out_b, w_pe, w_pg, final_g):
    h = x
    conv_new = []
    pool_new = []
    for i in range(DEPTH):
        hn = rmsnorm(h, norm_g[i])
        k = i // N_MIXERS
        if i % N_MIXERS == 0:
            out, st = conv_mixer(hn, conv_hist[k], w_in_a[k], conv_w_a[k], w_out_a[k])
            conv_new.append(st)
        else:
            out, st = pool_mixer(hn, pool_hist[k], pos0, w_in_b[k], w_grp_b[k],
                                 scale_b[k], w_out_b[k])
            pool_new.append(st)
        h = h + out
        h = h + (p[i] @ w_pe[i]) * jax.nn.sigmoid(h @ w_pg[i])
    return rmsnorm(h, final_g), jnp.stack(conv_new), jnp.stack(pool_new)


def setup_inputs(seed: int = 0) -> dict:
    key = jax.random.key(seed)
    ks = jax.random.split(key, 20)
    f32 = jnp.float32
    nrm = lambda k, s, sc: jax.random.normal(k, s, f32) * sc
    return {
        "x_prompt": nrm(ks[0], (BATCH, SEQ, D_MODEL), 1.0),
        "x_sample": nrm(ks[1], (DEC_BATCH, DEC_SEQ, D_MODEL), 1.0),
        "state_conv": nrm(ks[2], (N_A, DEC_BATCH, CONV_W - 1, E_A), 1.0),
        "state_pool": nrm(ks[3], (N_B, DEC_BATCH, POOL_HIST, E_B), 1.0),
        "p_prompt": nrm(ks[4], (DEPTH, BATCH, SEQ, P_DIM), 1.0),
        "p_sample": nrm(ks[5], (DEPTH, DEC_BATCH, DEC_SEQ, P_DIM), 1.0),
        "norm_g": 1.0 + nrm(ks[6], (DEPTH, D_MODEL), 0.05),
        "w_in_a": nrm(ks[7], (N_A, D_MODEL, 4 * E_A), D_MODEL ** -0.5),
        "conv_w_a": nrm(ks[8], (N_A, CONV_W, E_A), CONV_W ** -0.5),
        "w_out_a": nrm(ks[9], (N_A, E_A, D_MODEL), 0.5 * E_A ** -0.5),
        "w_in_b": nrm(ks[10], (N_B, D_MODEL, 2 * E_B), D_MODEL ** -0.5),
        "w_grp_b": nrm(ks[11], (N_B, N_POOL_GROUPS, G_B, G_B), G_B ** -0.5),
        "scale_b": 1.0 + nrm(ks[12], (N_B, E_B), 0.1),
        "w_out_b": nrm(ks[13], (N_B, E_B, D_MODEL), 0.5 * E_B ** -0.5),
        "w_pe": nrm(ks[14], (DEPTH, P_DIM, D_MODEL), P_DIM ** -0.5),
        "w_pg": nrm(ks[15], (DEPTH, D_MODEL, D_MODEL), D_MODEL ** -0.5),
        "final_g": 1.0 + nrm(ks[16], (D_MODEL,), 0.05),
    }


def reference(x_prompt, x_sample, state_conv, state_pool, p_prompt, p_sample, norm_g,
              w_in_a, conv_w_a, w_out_a, w_in_b, w_grp_b, scale_b, w_out_b,
              w_pe, w_pg, final_g):
    bp = x_prompt.shape[0]
    conv0 = jnp.zeros((N_A, bp, CONV_W - 1, E_A), x_prompt.dtype)
    pool0 = jnp.zeros((N_B, bp, POOL_HIST, E_B), x_prompt.dtype)
    y_prompt, conv_state_prompt, pool_state_prompt = trunk(
        x_prompt, p_prompt, conv0, pool0, 0, norm_g, w_in_a, conv_w_a, w_out_a,
        w_in_b, w_grp_b, scale_b, w_out_b, w_pe, w_pg, final_g)
    y_sample, conv_state_sample, pool_state_sample = trunk(
        x_sample, p_sample, state_conv, state_pool, PAST_LEN, norm_g, w_in_a, conv_w_a,
        w_out_a, w_in_b, w_grp_b, scale_b, w_out_b, w_pe, w_pg, final_g)
    return (y_prompt, y_sample, conv_state_prompt, pool_state_prompt,
            conv_state_sample, pool_state_sample)
```

```python
import functools

import jax
import jax.numpy as jnp
from jax import lax
from jax.experimental import pallas as pl
from jax.experimental.pallas import tpu as pltpu

D_MODEL = 1024
E_BRANCH = 2048
P_DIM = 256
CONV_W = 3
POOL_WINDOWS = (2, 4, 8, 16)
G_POOL = E_BRANCH // len(POOL_WINDOWS)
POOL_HIST = max(POOL_WINDOWS) - 1
PAST_LEN = 2048
EPS = 1e-6

SUBLANES = 8
CONV_PAD = SUBLANES
POOL_PAD = 2 * SUBLANES
CHUNK = 512
VMEM_LIMIT_BYTES = 56 * 1024 * 1024

_BF16 = jnp.bfloat16
_F32 = jnp.float32


def _dot(a, b):
    return jnp.dot(a, b, preferred_element_type=_F32)


def _rmsnorm(x, g):
    r = lax.rsqrt(jnp.mean(x * x, axis=-1, keepdims=True) + EPS)
    return (x * r) * g


def _silu(z):
    return z * jax.nn.sigmoid(z)


def _epilogue(x, mix_out, p_ref, w_pe_ref, w_pg_ref, fg_ref, o_ref, ns, tl, final):
    h = x + mix_out
    p = p_ref[...].reshape(ns * tl, P_DIM).astype(_BF16)
    pe = _dot(p, w_pe_ref[...])
    gate = jax.nn.sigmoid(_dot(h.astype(_BF16), w_pg_ref[...]))
    h = h + pe * gate
    if final:
        h = _rmsnorm(h, fg_ref[...])
    o_ref[...] = h.reshape(ns, tl, D_MODEL)


def _conv_layer_kernel(x_ref, p_ref, hist_ref, g_ref, w_in_ref, cw_ref, w_out_ref,
                       w_pe_ref, w_pg_ref, fg_ref, o_ref, st_ref, vbuf, ybuf,
                       *, ns, tl, final):
    t = pl.program_id(1)

    @pl.when(t == 0)
    def _():
        vbuf[:, 0:CONV_PAD, :] = hist_ref[...]

    x = x_ref[...].reshape(ns * tl, D_MODEL)
    hn = _rmsnorm(x, g_ref[...]).astype(_BF16)

    for c in range(E_BRANCH // CHUNK):
        lo = c * CHUNK
        sl = slice(lo, lo + CHUNK)
        b_gate = _dot(hn, w_in_ref[:, 0 * E_BRANCH + lo:0 * E_BRANCH + lo + CHUNK])
        c_gate = _dot(hn, w_in_ref[:, 1 * E_BRANCH + lo:1 * E_BRANCH + lo + CHUNK])
        hh = _dot(hn, w_in_ref[:, 2 * E_BRANCH + lo:2 * E_BRANCH + lo + CHUNK])
        zz = _dot(hn, w_in_ref[:, 3 * E_BRANCH + lo:3 * E_BRANCH + lo + CHUNK])
        v = c_gate * hh
        cw = cw_ref[:, sl]
        convs = []
        for s in range(ns):
            vbuf[s, CONV_PAD:CONV_PAD + tl, sl] = v[s * tl:(s + 1) * tl]
            y = cw[CONV_W - 1:CONV_W] * v[s * tl:(s + 1) * tl]
            for k in range(1, CONV_W):
                y = y + cw[CONV_W - 1 - k:CONV_W - k] * vbuf[s, CONV_PAD - k:CONV_PAD - k + tl, sl]
            convs.append(y)
        conv = convs[0] if ns == 1 else jnp.concatenate(convs, axis=0)
        ybuf[:, sl] = (b_gate * conv * _silu(zz)).astype(_BF16)

    st_ref[...] = vbuf[:, CONV_PAD + tl - (CONV_W - 1):CONV_PAD + tl, :]
    vbuf[:, 0:CONV_PAD, :] = vbuf[:, tl:tl + CONV_PAD, :]

    mix_out = _dot(ybuf[...], w_out_ref[...])
    _epilogue(x, mix_out, p_ref, w_pe_ref, w_pg_ref, fg_ref, o_ref, ns, tl, final)


def _pool_layer_kernel(x_ref, p_ref, hist_ref, g_ref, w_in_ref, w_grp_ref, scale_ref,
                       w_out_ref, w_pe_ref, w_pg_ref, fg_ref, o_ref, st_ref, ubuf, ybuf,
                       *, ns, tl, pos0, final):
    t = pl.program_id(1)

    @pl.when(t == 0)
    def _():
        ubuf[:, 0:POOL_PAD, :] = hist_ref[...]

    x = x_ref[...].reshape(ns * tl, D_MODEL)
    hn = _rmsnorm(x, g_ref[...]).astype(_BF16)

    pos = pos0 + t * tl + lax.broadcasted_iota(jnp.int32, (tl, 128), 0)

    for j, w in enumerate(POOL_WINDOWS):
        lo = j * G_POOL
        sl = slice(lo, lo + G_POOL)
        u = _dot(hn, w_in_ref[:, lo:lo + G_POOL])
        zz = _dot(hn, w_in_ref[:, E_BRANCH + lo:E_BRANCH + lo + G_POOL])
        inv_cnt = 1.0 / jnp.minimum(pos + 1, w).astype(_F32)
        inv_cnt = jnp.concatenate([inv_cnt] * (G_POOL // 128), axis=1)
        ds = []
        for s in range(ns):
            us = u[s * tl:(s + 1) * tl]
            ubuf[s, POOL_PAD:POOL_PAD + tl, sl] = us
            acc = us
            for k in range(1, w):
                acc = acc + ubuf[s, POOL_PAD - k:POOL_PAD - k + tl, sl]
            ds.append(acc * inv_cnt - us)
        d = ds[0] if ns == 1 else jnp.concatenate(ds, axis=0)
        mixed = _dot(d.astype(_BF16), w_grp_ref[j]) * scale_ref[:, sl]
        ybuf[:, sl] = (mixed * _silu(zz)).astype(_BF16)

    st_ref[...] = ubuf[:, POOL_PAD + tl - POOL_HIST:POOL_PAD + tl, :]
    ubuf[:, 0:POOL_PAD, :] = ubuf[:, tl:tl + POOL_PAD, :]

    mix_out = _dot(ybuf[...], w_out_ref[...])
    _epilogue(x, mix_out, p_ref, w_pe_ref, w_pg_ref, fg_ref, o_ref, ns, tl, final)


def _const_spec(shape):
    zeros = (0,) * len(shape)
    return pl.BlockSpec(shape, lambda b, t: zeros, pipeline_mode=pl.Buffered(1))


def _layer_call(kind, layer, x, p_all, hist, g, mixer_w, w_pe, w_pg, final_g, *,
                ns, tl, pos0, final):
    n, seq, _ = x.shape
    grid = (n // ns, seq // tl)
    pad = hist.shape[1]
    n_state = CONV_W - 1 if kind == "conv" else POOL_HIST

    if kind == "conv":
        body = functools.partial(_conv_layer_kernel, ns=ns, tl=tl, final=final)
    else:
        body = functools.partial(_pool_layer_kernel, ns=ns, tl=tl, pos0=pos0, final=final)

    in_specs = [
        pl.BlockSpec((ns, tl, D_MODEL), lambda b, t: (b, t, 0)),
        pl.BlockSpec((None, ns, tl, P_DIM), lambda b, t: (layer, b, t, 0)),
        pl.BlockSpec((ns, pad, E_BRANCH), lambda b, t: (b, 0, 0)),
        _const_spec(g.shape),
    ]
    in_specs += [_const_spec(w.shape) for w in mixer_w]
    in_specs += [_const_spec(w_pe.shape), _const_spec(w_pg.shape), _const_spec(final_g.shape)]

    out_shape = (jax.ShapeDtypeStruct((n, seq, D_MODEL), _F32),
                 jax.ShapeDtypeStruct((n, n_state, E_BRANCH), _F32))
    out_specs = (pl.BlockSpec((ns, tl, D_MODEL), lambda b, t: (b, t, 0)),
                 pl.BlockSpec((ns, n_state, E_BRANCH), lambda b, t: (b, 0, 0)))
    scratch = [pltpu.VMEM((ns, pad + tl, E_BRANCH), _F32),
               pltpu.VMEM((ns * tl, E_BRANCH), _BF16)]

    return pl.pallas_call(
        body,
        grid=grid,
        in_specs=in_specs,
        out_specs=out_specs,
        out_shape=out_shape,
        scratch_shapes=scratch,
        compiler_params=pltpu.CompilerParams(
            dimension_semantics=("arbitrary", "arbitrary"),
            vmem_limit_bytes=VMEM_LIMIT_BYTES),
        name=f"{kind}_layer{layer}_{'sample' if pos0 else 'prompt'}",
    )(x, p_all, hist, g, *mixer_w, w_pe, w_pg, final_g)


def _trunk(x, p_all, conv_hist, pool_hist, pos0, weights, *, ns, tl):
    (norm_g, w_in_a, conv_w_a, w_out_a, w_in_b, w_grp_b, scale_b, w_out_b,
     w_pe, w_pg, final_g) = weights
    depth = norm_g.shape[0]
    h = x
    conv_new, pool_new = [], []
    for i in range(depth):
        k = i // 2
        common = dict(ns=ns, tl=tl, pos0=pos0, final=(i == depth - 1))
        if i % 2 == 0:
            h, st = _layer_call("conv", i, h, p_all, conv_hist[k], norm_g[i:i + 1],
                                (w_in_a[k], conv_w_a[k], w_out_a[k]),
                                w_pe[i], w_pg[i], final_g, **common)
            conv_new.append(st)
        else:
            h, st = _layer_call("pool", i, h, p_all, pool_hist[k], norm_g[i:i + 1],
                                (w_in_b[k], w_grp_b[k], scale_b[k:k + 1], w_out_b[k]),
                                w_pe[i], w_pg[i], final_g, **common)
            pool_new.append(st)
    return h, jnp.stack(conv_new), jnp.stack(pool_new)


def kernel(x_prompt, x_sample, state_conv, state_pool, p_prompt, p_sample, norm_g, w_in_a, conv_w_a, w_out_a, w_in_b, w_grp_b, scale_b, w_out_b, w_pe, w_pg, final_g):
    n_a, n_b = state_conv.shape[0], state_pool.shape[0]
    bp, seq, _ = x_prompt.shape
    bs, dec_seq, _ = x_sample.shape

    weights = (norm_g, w_in_a.astype(_BF16), conv_w_a, w_out_a.astype(_BF16),
               w_in_b.astype(_BF16), w_grp_b.astype(_BF16), scale_b, w_out_b.astype(_BF16),
               w_pe.astype(_BF16), w_pg.astype(_BF16), final_g.reshape(1, D_MODEL))

    conv0 = jnp.zeros((n_a, bp, CONV_PAD, E_BRANCH), _F32)
    pool0 = jnp.zeros((n_b, bp, POOL_PAD, E_BRANCH), _F32)
    conv_s = jnp.pad(state_conv, ((0, 0), (0, 0), (CONV_PAD - (CONV_W - 1), 0), (0, 0)))
    pool_s = jnp.pad(state_pool, ((0, 0), (0, 0), (POOL_PAD - POOL_HIST, 0), (0, 0)))

    y_p, conv_p, pool_p = _trunk(x_prompt, p_prompt, conv0, pool0, 0, weights,
                                 ns=1, tl=512)
    y_s, conv_sn, pool_sn = _trunk(x_sample, p_sample, conv_s, pool_s, PAST_LEN, weights,
                                   ns=bs, tl=dec_seq)
    return (y_p, y_s, conv_p, pool_p, conv_sn, pool_sn)
```

```python
import functools

import jax
import jax.numpy as jnp
from jax import lax
from jax.experimental import pallas as pl
from jax.experimental.pallas import tpu as pltpu

D_MODEL = 1024
E_BRANCH = 2048
P_DIM = 256
CONV_W = 3
N_CONV_GATES = 4
POOL_WINDOWS = (2, 4, 8, 16)
G_POOL = E_BRANCH // len(POOL_WINDOWS)
POOL_HIST = max(POOL_WINDOWS) - 1
PAST_LEN = 2048
EPS = 1e-6

SUBLANES = 8
LANES = 128
CONV_PAD = SUBLANES
POOL_PAD = 2 * SUBLANES
SLAB = 512
TILE_ROWS = 1024
SUB_ROWS = 256
V7X_VMEM_BYTES = 64 * 1024 * 1024
VMEM_SPILL_RESERVE = 8 * 1024 * 1024

_BF16 = jnp.bfloat16
_F32 = jnp.float32


def _dot(a, b):
    return jnp.dot(a, b, preferred_element_type=_F32)


def _rmsnorm(x, g):
    r = lax.rsqrt(jnp.mean(x * x, axis=-1, keepdims=True) + EPS)
    return (x * r) * g


def _silu(z):
    return z * jax.nn.sigmoid(z)


def _shift_rows(a, k):
    return pltpu.roll(a, k, 0)


def _epilogue(x, y, p, w_out_ref, w_pe_ref, w_pg_ref, fg_ref, final):
    n_slab = D_MODEL // SLAB
    h = jnp.concatenate([x[:, i * SLAB:(i + 1) * SLAB] + _dot(y, w_out_ref[i])
                         for i in range(n_slab)], axis=1)
    h_bf = h.astype(_BF16)
    p_bf = p.astype(_BF16)
    h = jnp.concatenate([h[:, i * SLAB:(i + 1) * SLAB]
                         + _dot(p_bf, w_pe_ref[i]) * jax.nn.sigmoid(_dot(h_bf, w_pg_ref[i]))
                         for i in range(n_slab)], axis=1)
    if final:
        h = _rmsnorm(h, fg_ref[...])
    return h


def _sub_tiles(x_ref, p_ref, ns, tl):
    if ns > 1:
        yield (slice(0, ns * tl), x_ref[...].reshape(ns * tl, D_MODEL),
               p_ref[...].reshape(ns * tl, P_DIM), tl)
        return
    rows = min(tl, SUB_ROWS)
    for r in range(tl // rows):
        rs = slice(r * rows, (r + 1) * rows)
        yield rs, x_ref[0, rs, :], p_ref[0, rs, :], rows


def _conv_layer_kernel(x_ref, p_ref, hist_ref, g_ref, w_in_ref, cw_ref, w_out_ref,
                       w_pe_ref, w_pg_ref, fg_ref, o_ref, st_ref, carry, ybuf,
                       *, ns, tl, final):
    @pl.when(pl.program_id(1) == 0)
    def _():
        carry[...] = hist_ref[...]

    n_blk = E_BRANCH // LANES
    hist = [carry[:, :, i * LANES:(i + 1) * LANES] for i in range(n_blk)]
    for rs, x, p, tls in _sub_tiles(x_ref, p_ref, ns, tl):
        hn = _rmsnorm(x, g_ref[...]).astype(_BF16)
        for i in range(n_blk):
            sl = slice(i * LANES, (i + 1) * LANES)
            gates = _dot(hn, w_in_ref[i])
            b_gate, c_gate, hh, zz = (gates[:, q * LANES:(q + 1) * LANES]
                                      for q in range(N_CONV_GATES))
            v = c_gate * hh
            cw = cw_ref[:, sl]
            convs, tails = [], []
            for s in range(ns):
                vp = jnp.concatenate([hist[i][s], v[s * tls:(s + 1) * tls]], axis=0)
                y = cw[CONV_W - 1:CONV_W] * vp
                for k in range(1, CONV_W):
                    y = y + cw[CONV_W - 1 - k:CONV_W - k] * _shift_rows(vp, k)
                convs.append(y[CONV_PAD:])
                tails.append(vp[tls:])
            hist[i] = jnp.stack(tails, axis=0)
            conv = convs[0] if ns == 1 else jnp.concatenate(convs, axis=0)
            ybuf[rs, sl] = (b_gate * conv * _silu(zz)).astype(_BF16)
        h = _epilogue(x, ybuf[rs, :], p, w_out_ref, w_pe_ref, w_pg_ref, fg_ref, final)
        if ns == 1:
            o_ref[0, rs, :] = h
        else:
            o_ref[...] = h.reshape(ns, tl, D_MODEL)

    for i in range(n_blk):
        carry[:, :, i * LANES:(i + 1) * LANES] = hist[i]
    st_ref[...] = carry[:, CONV_PAD - (CONV_W - 1):, :]


def _pool_layer_kernel(x_ref, p_ref, hist_ref, g_ref, w_in_ref, w_grp_ref, scale_ref,
                       w_out_ref, w_pe_ref, w_pg_ref, fg_ref, o_ref, st_ref, carry, ybuf,
                       *, ns, tl, pos0, final):
    t = pl.program_id(1)

    @pl.when(t == 0)
    def _():
        carry[...] = hist_ref[...]

    n_grp = len(POOL_WINDOWS)
    hist = [carry[:, :, j * G_POOL:(j + 1) * G_POOL] for j in range(n_grp)]
    for rs, x, p, tls in _sub_tiles(x_ref, p_ref, ns, tl):
        hn = _rmsnorm(x, g_ref[...]).astype(_BF16)
        row0 = 0 if ns > 1 else rs.start
        pos = pos0 + t * tl + row0 + lax.broadcasted_iota(jnp.int32, (tls, LANES), 0)
        for j, w in enumerate(POOL_WINDOWS):
            sl = slice(j * G_POOL, (j + 1) * G_POOL)
            u = _dot(hn, w_in_ref[j])
            zz = _dot(hn, w_in_ref[n_grp + j])
            inv_cnt = 1.0 / jnp.minimum(pos + 1, w).astype(_F32)
            inv_cnt = jnp.concatenate([inv_cnt] * (G_POOL // LANES), axis=1)
            ds, tails = [], []
            for s in range(ns):
                us = u[s * tls:(s + 1) * tls]
                up = jnp.concatenate([hist[j][s], us], axis=0)
                acc = up
                k = 1
                while k < w:
                    acc = acc + _shift_rows(acc, k)
                    k *= 2
                ds.append(acc[POOL_PAD:] * inv_cnt - us)
                tails.append(up[tls:])
            hist[j] = jnp.stack(tails, axis=0)
            d = ds[0] if ns == 1 else jnp.concatenate(ds, axis=0)
            mixed = _dot(d.astype(_BF16), w_grp_ref[j]) * scale_ref[:, sl]
            ybuf[rs, sl] = (mixed * _silu(zz)).astype(_BF16)
        h = _epilogue(x, ybuf[rs, :], p, w_out_ref, w_pe_ref, w_pg_ref, fg_ref, final)
        if ns == 1:
            o_ref[0, rs, :] = h
        else:
            o_ref[...] = h.reshape(ns, tl, D_MODEL)

    for j in range(n_grp):
        carry[:, :, j * G_POOL:(j + 1) * G_POOL] = hist[j]
    st_ref[...] = carry[:, POOL_PAD - POOL_HIST:, :]


def _slabs(w):
    n_l, k, n = w.shape
    return w.astype(_BF16).reshape(n_l, k, n // SLAB, SLAB).transpose(0, 2, 1, 3)


def _gate_slabs(w):
    n_l, k, n = w.shape
    gates = n // E_BRANCH
    w = w.astype(_BF16).reshape(n_l, k, gates, E_BRANCH // LANES, LANES)
    return w.transpose(0, 3, 1, 2, 4).reshape(n_l, E_BRANCH // LANES, k, gates * LANES)


def _layer_spec(arr, idx):
    zeros = (0,) * (arr.ndim - 1)
    return pl.BlockSpec((None,) + arr.shape[1:], lambda b, t: (idx,) + zeros,
                        pipeline_mode=pl.Buffered(1))


def _nbytes(shape, dtype):
    n = jnp.dtype(dtype).itemsize
    for d in shape:
        n *= d
    return n


def _layer_call(kind, layer, x, p_all, hist_all, norm_g, mixer_w, w_pe, w_pg, final_g, *,
                ns, tl, pos0, final):
    n, seq, _ = x.shape
    grid = (n // ns, seq // tl)
    k = layer // 2
    pad = hist_all.shape[2]
    n_state = CONV_W - 1 if kind == "conv" else POOL_HIST

    if kind == "conv":
        body = functools.partial(_conv_layer_kernel, ns=ns, tl=tl, final=final)
    else:
        body = functools.partial(_pool_layer_kernel, ns=ns, tl=tl, pos0=pos0, final=final)

    params = (norm_g,) + tuple(mixer_w) + (w_pe, w_pg, final_g)
    param_idx = (layer,) + (k,) * len(mixer_w) + (layer, layer, 0)
    in_specs = [
        pl.BlockSpec((ns, tl, D_MODEL), lambda b, t: (b, t, 0)),
        pl.BlockSpec((None, ns, tl, P_DIM), lambda b, t: (layer, b, t, 0)),
        pl.BlockSpec((None, ns, pad, E_BRANCH), lambda b, t: (k, b, 0, 0)),
    ] + [_layer_spec(w, i) for w, i in zip(params, param_idx)]

    out_shape = (jax.ShapeDtypeStruct((n, seq, D_MODEL), _F32),
                 jax.ShapeDtypeStruct((n, n_state, E_BRANCH), _F32))
    out_specs = (pl.BlockSpec((ns, tl, D_MODEL), lambda b, t: (b, t, 0)),
                 pl.BlockSpec((ns, n_state, E_BRANCH), lambda b, t: (b, 0, 0)))
    scratch = [pltpu.VMEM((ns, pad, E_BRANCH), _F32),
               pltpu.VMEM((ns * tl, E_BRANCH), _BF16)]

    pipelined = (2 * _nbytes((ns, tl, D_MODEL), _F32) + _nbytes((ns, tl, P_DIM), _F32)
                 + _nbytes((ns, pad, E_BRANCH), _F32) + _nbytes((ns, n_state, E_BRANCH), _F32))
    vmem_bytes = (sum(_nbytes(w.shape[1:], w.dtype) for w in params) + 2 * pipelined
                  + _nbytes((ns, pad, E_BRANCH), _F32) + _nbytes((ns * tl, E_BRANCH), _BF16)
                  + VMEM_SPILL_RESERVE)
    assert vmem_bytes <= V7X_VMEM_BYTES, vmem_bytes

    return pl.pallas_call(
        body,
        grid=grid,
        in_specs=in_specs,
        out_specs=out_specs,
        out_shape=out_shape,
        scratch_shapes=scratch,
        compiler_params=pltpu.CompilerParams(
            dimension_semantics=("arbitrary", "arbitrary"),
            vmem_limit_bytes=vmem_bytes),
        name=f"{kind}_layer{layer}_{'sample' if pos0 else 'prompt'}",
    )(x, p_all, hist_all, *params)


def _trunk(x, p_all, conv_hist, pool_hist, pos0, weights, *, ns, tl):
    (norm_g, conv_w, pool_w, w_pe, w_pg, final_g) = weights
    depth = norm_g.shape[0]
    h = x
    conv_new, pool_new = [], []
    for i in range(depth):
        common = dict(ns=ns, tl=tl, pos0=pos0, final=(i == depth - 1))
        if i % 2 == 0:
            h, st = _layer_call("conv", i, h, p_all, conv_hist, norm_g, conv_w,
                                w_pe, w_pg, final_g, **common)
            conv_new.append(st)
        else:
            h, st = _layer_call("pool", i, h, p_all, pool_hist, norm_g, pool_w,
                                w_pe, w_pg, final_g, **common)
            pool_new.append(st)
    return h, jnp.stack(conv_new), jnp.stack(pool_new)


def kernel(x_prompt, x_sample, state_conv, state_pool, p_prompt, p_sample, norm_g, w_in_a, conv_w_a, w_out_a, w_in_b, w_grp_b, scale_b, w_out_b, w_pe, w_pg, final_g):
    n_a, n_b = state_conv.shape[0], state_pool.shape[0]
    bp, seq, _ = x_prompt.shape
    bs, dec_seq, _ = x_sample.shape

    weights = (norm_g[:, None, :],
               (_gate_slabs(w_in_a), conv_w_a, _slabs(w_out_a)),
               (_slabs(w_in_b), w_grp_b.astype(_BF16), scale_b[:, None, :], _slabs(w_out_b)),
               _slabs(w_pe), _slabs(w_pg), final_g[None, None, :])

    conv0 = jnp.zeros((n_a, bp, CONV_PAD, E_BRANCH), _F32)
    pool0 = jnp.zeros((n_b, bp, POOL_PAD, E_BRANCH), _F32)
    conv_s = jnp.pad(state_conv, ((0, 0), (0, 0), (CONV_PAD - (CONV_W - 1), 0), (0, 0)))
    pool_s = jnp.pad(state_pool, ((0, 0), (0, 0), (POOL_PAD - POOL_HIST, 0), (0, 0)))

    y_p, conv_p, pool_p = _trunk(x_prompt, p_prompt, conv0, pool0, 0, weights,
                                 ns=1, tl=min(TILE_ROWS, seq))
    y_s, conv_sn, pool_sn = _trunk(x_sample, p_sample, conv_s, pool_s, PAST_LEN, weights,
                                   ns=bs, tl=dec_seq)
    return (y_p, y_s, conv_p, pool_p, conv_sn, pool_sn)
```

```python
import functools

import jax
import jax.numpy as jnp
from jax import lax
from jax.experimental import pallas as pl
from jax.experimental.pallas import tpu as pltpu

D_MODEL = 1024
E_BRANCH = 2048
P_DIM = 256
CONV_W = 3
N_CONV_GATES = 4
POOL_WINDOWS = (2, 4, 8, 16)
G_POOL = E_BRANCH // len(POOL_WINDOWS)
POOL_HIST = max(POOL_WINDOWS) - 1
PAST_LEN = 2048
EPS = 1e-6

SUBLANES = 8
LANES = 128
CONV_PAD = SUBLANES
POOL_PAD = 2 * SUBLANES
SLAB = 512
TILE_ROWS = 1024
SUB_ROWS = 256
V7X_VMEM_BYTES = 64 * 1024 * 1024
VMEM_SPILL_RESERVE = 8 * 1024 * 1024

_BF16 = jnp.bfloat16
_F32 = jnp.float32


def _dot(a, b):
    return jnp.dot(a, b, preferred_element_type=_F32)


def _rmsnorm(x, g):
    r = lax.rsqrt(jnp.mean(x * x, axis=-1, keepdims=True) + EPS)
    return (x * r) * g


def _silu(z):
    return z * jax.nn.sigmoid(z)


def _shift_rows(a, k):
    return pltpu.roll(a, k, 0)


def _epilogue(x, y, p, w_out_refs, w_pe_refs, w_pg_refs, fg_ref, final):
    n_slab = D_MODEL // SLAB
    h = jnp.concatenate([x[:, i * SLAB:(i + 1) * SLAB] + _dot(y, w_out_refs[i][...])
                         for i in range(n_slab)], axis=1)
    h_bf = h.astype(_BF16)
    p_bf = p.astype(_BF16)
    h = jnp.concatenate([h[:, i * SLAB:(i + 1) * SLAB]
                         + _dot(p_bf, w_pe_refs[i][...]) * jax.nn.sigmoid(_dot(h_bf, w_pg_refs[i][...]))
                         for i in range(n_slab)], axis=1)
    if final:
        h = _rmsnorm(h, fg_ref[...])
    return h


def _sub_tiles(x_ref, p_ref, ns, tl):
    if ns > 1:
        yield (slice(0, ns * tl), x_ref[...].reshape(ns * tl, D_MODEL),
               p_ref[...].reshape(ns * tl, P_DIM), tl)
        return
    rows = min(tl, SUB_ROWS)
    for r in range(tl // rows):
        rs = slice(r * rows, (r + 1) * rows)
        yield rs, x_ref[0, rs, :], p_ref[0, rs, :], rows


def _split_refs(refs, counts):
    out, at = [], 0
    for c in counts:
        if c is None:
            out.append(refs[at])
            at += 1
        else:
            out.append(refs[at:at + c])
            at += c
    assert at == len(refs), (at, len(refs))
    return out


_N_OUT_SLABS = D_MODEL // SLAB
_TAIL_COUNTS = (_N_OUT_SLABS, _N_OUT_SLABS, _N_OUT_SLABS, None, None, None, None, None)


def _conv_layer_kernel(*refs, ns, tl, final):
    (x_ref, p_ref, hist_ref, g_ref, w_in_ref, cw_ref, w_out_refs, w_pe_refs, w_pg_refs, fg_ref,
     o_ref, st_ref, carry, ybuf) = _split_refs(refs, (None,) * 6 + _TAIL_COUNTS)
    @pl.when(pl.program_id(1) == 0)
    def _():
        carry[...] = hist_ref[...]

    n_blk = E_BRANCH // LANES
    hist = [carry[:, :, i * LANES:(i + 1) * LANES] for i in range(n_blk)]
    for rs, x, p, tls in _sub_tiles(x_ref, p_ref, ns, tl):
        hn = _rmsnorm(x, g_ref[...]).astype(_BF16)
        for i in range(n_blk):
            sl = slice(i * LANES, (i + 1) * LANES)
            gates = _dot(hn, w_in_ref[i])
            b_gate, c_gate, hh, zz = (gates[:, q * LANES:(q + 1) * LANES]
                                      for q in range(N_CONV_GATES))
            v = c_gate * hh
            cw = cw_ref[:, sl]
            convs, tails = [], []
            for s in range(ns):
                vp = jnp.concatenate([hist[i][s], v[s * tls:(s + 1) * tls]], axis=0)
                y = cw[CONV_W - 1:CONV_W] * vp
                for k in range(1, CONV_W):
                    y = y + cw[CONV_W - 1 - k:CONV_W - k] * _shift_rows(vp, k)
                convs.append(y[CONV_PAD:])
                tails.append(vp[tls:])
            hist[i] = jnp.stack(tails, axis=0)
            conv = convs[0] if ns == 1 else jnp.concatenate(convs, axis=0)
            ybuf[rs, sl] = (b_gate * conv * _silu(zz)).astype(_BF16)
        h = _epilogue(x, ybuf[rs, :], p, w_out_refs, w_pe_refs, w_pg_refs, fg_ref, final)
        if ns == 1:
            o_ref[0, rs, :] = h
        else:
            o_ref[...] = h.reshape(ns, tl, D_MODEL)

    for i in range(n_blk):
        carry[:, :, i * LANES:(i + 1) * LANES] = hist[i]
    st_ref[...] = carry[:, CONV_PAD - (CONV_W - 1):, :]


def _pool_layer_kernel(*refs, ns, tl, pos0, final):
    n_in_slabs = 2 * E_BRANCH // SLAB
    (x_ref, p_ref, hist_ref, g_ref, w_in_refs, w_grp_ref, scale_ref, w_out_refs, w_pe_refs,
     w_pg_refs, fg_ref, o_ref, st_ref, carry, ybuf) = _split_refs(
         refs, (None,) * 4 + (n_in_slabs, None, None) + _TAIL_COUNTS)
    t = pl.program_id(1)

    @pl.when(t == 0)
    def _():
        carry[...] = hist_ref[...]

    n_grp = len(POOL_WINDOWS)
    hist = [carry[:, :, j * G_POOL:(j + 1) * G_POOL] for j in range(n_grp)]
    for rs, x, p, tls in _sub_tiles(x_ref, p_ref, ns, tl):
        hn = _rmsnorm(x, g_ref[...]).astype(_BF16)
        row0 = 0 if ns > 1 else rs.start
        pos = pos0 + t * tl + row0 + lax.broadcasted_iota(jnp.int32, (tls, LANES), 0)
        for j, w in reversed(list(enumerate(POOL_WINDOWS))):
            sl = slice(j * G_POOL, (j + 1) * G_POOL)
            u = _dot(hn, w_in_refs[j][...])
            zz = _dot(hn, w_in_refs[n_grp + j][...])
            inv_cnt = 1.0 / jnp.minimum(pos + 1, w).astype(_F32)
            inv_cnt = jnp.concatenate([inv_cnt] * (G_POOL // LANES), axis=1)
            ds, tails = [], []
            for s in range(ns):
                us = u[s * tls:(s + 1) * tls]
                up = jnp.concatenate([hist[j][s], us], axis=0)
                acc = up
                k = 1
                while k < w:
                    acc = acc + _shift_rows(acc, k)
                    k *= 2
                ds.append(acc[POOL_PAD:] * inv_cnt - us)
                tails.append(up[tls:])
            hist[j] = jnp.stack(tails, axis=0)
            d = ds[0] if ns == 1 else jnp.concatenate(ds, axis=0)
            mixed = _dot(d.astype(_BF16), w_grp_ref[j]) * scale_ref[:, sl]
            ybuf[rs, sl] = (mixed * _silu(zz)).astype(_BF16)
        h = _epilogue(x, ybuf[rs, :], p, w_out_refs, w_pe_refs, w_pg_refs, fg_ref, final)
        if ns == 1:
            o_ref[0, rs, :] = h
        else:
            o_ref[...] = h.reshape(ns, tl, D_MODEL)

    for j in range(n_grp):
        carry[:, :, j * G_POOL:(j + 1) * G_POOL] = hist[j]
    st_ref[...] = carry[:, POOL_PAD - POOL_HIST:, :]


def _gate_slab_kernel(w_ref, o_ref):
    for i in range(E_BRANCH // LANES):
        for q in range(N_CONV_GATES):
            src = q * E_BRANCH + i * LANES
            o_ref[i, :, q * LANES:(q + 1) * LANES] = w_ref[:, src:src + LANES].astype(_BF16)


def _nbytes(shape, dtype):
    n = jnp.dtype(dtype).itemsize
    for d in shape:
        n *= d
    return n


def _gate_slabs(w):
    n_l, k, n = w.shape
    rows = min(k, SUB_ROWS)
    n_blk = E_BRANCH // LANES
    vmem_bytes = 2 * (_nbytes((rows, n), _F32) + _nbytes((rows, n), _BF16)) + VMEM_SPILL_RESERVE
    return pl.pallas_call(
        _gate_slab_kernel,
        grid=(n_l, k // rows),
        in_specs=[pl.BlockSpec((None, rows, n), lambda a, r: (a, r, 0))],
        out_specs=pl.BlockSpec((None, n_blk, rows, n // n_blk), lambda a, r: (a, 0, r, 0)),
        out_shape=jax.ShapeDtypeStruct((n_l, n_blk, k, n // n_blk), _BF16),
        compiler_params=pltpu.CompilerParams(
            dimension_semantics=("arbitrary", "arbitrary"), vmem_limit_bytes=vmem_bytes),
        name="conv_gate_slabs",
    )(w)


def _whole(arr, idx):
    zeros = (0,) * (arr.ndim - 1)
    return [(arr, pl.BlockSpec((None,) + arr.shape[1:], lambda b, t: (idx,) + zeros,
                               pipeline_mode=pl.Buffered(1)))]


def _col_slabs(arr, idx):
    _, k, n = arr.shape
    return [(arr, pl.BlockSpec((None, k, SLAB), lambda b, t, c=c: (idx, 0, c),
                               pipeline_mode=pl.Buffered(1)))
            for c in range(n // SLAB)]


def _layer_call(kind, layer, x, p_all, hist_all, weights, *, ns, tl, pos0, final):
    (norm_g, (w_in_a, conv_w_a, w_out_a), (w_in_b, w_grp_b, scale_b, w_out_b),
     w_pe, w_pg, final_g) = weights
    n, seq, _ = x.shape
    grid = (n // ns, seq // tl)
    k = layer // 2
    pad = hist_all.shape[2]

    if kind == "conv":
        n_state = CONV_W - 1
        body = functools.partial(_conv_layer_kernel, ns=ns, tl=tl, final=final)
        mixer = _whole(w_in_a, k) + _whole(conv_w_a, k)
        w_out = w_out_a
    else:
        n_state = POOL_HIST
        body = functools.partial(_pool_layer_kernel, ns=ns, tl=tl, pos0=pos0, final=final)
        mixer = _col_slabs(w_in_b, k) + _whole(w_grp_b, k) + _whole(scale_b, k)
        w_out = w_out_b
    params = (_whole(norm_g, layer) + mixer + _col_slabs(w_out, k) + _col_slabs(w_pe, layer)
              + _col_slabs(w_pg, layer) + _whole(final_g, 0))

    tiles = [
        (x, pl.BlockSpec((ns, tl, D_MODEL), lambda b, t: (b, t, 0))),
        (p_all, pl.BlockSpec((None, ns, tl, P_DIM), lambda b, t: (layer, b, t, 0))),
        (hist_all, pl.BlockSpec((None, ns, pad, E_BRANCH), lambda b, t: (k, b, 0, 0))),
    ]
    out_shape = (jax.ShapeDtypeStruct((n, seq, D_MODEL), _F32),
                 jax.ShapeDtypeStruct((n, n_state, E_BRANCH), _F32))
    out_specs = (pl.BlockSpec((ns, tl, D_MODEL), lambda b, t: (b, t, 0)),
                 pl.BlockSpec((ns, n_state, E_BRANCH), lambda b, t: (b, 0, 0)))
    scratch = [pltpu.VMEM((ns, pad, E_BRANCH), _F32),
               pltpu.VMEM((ns * tl, E_BRANCH), _BF16)]

    def block_bytes(arr, spec):
        return _nbytes([d for d in spec.block_shape if d is not None], arr.dtype)

    vmem_bytes = (sum(block_bytes(a, s) for a, s in params)
                  + 2 * sum(block_bytes(a, s) for a, s in tiles)
                  + 2 * sum(block_bytes(a, s) for a, s in zip(out_shape, out_specs))
                  + _nbytes((ns, pad, E_BRANCH), _F32) + _nbytes((ns * tl, E_BRANCH), _BF16)
                  + VMEM_SPILL_RESERVE)
    assert vmem_bytes <= V7X_VMEM_BYTES, vmem_bytes

    operands = tiles + params
    return pl.pallas_call(
        body,
        grid=grid,
        in_specs=[s for _, s in operands],
        out_specs=out_specs,
        out_shape=out_shape,
        scratch_shapes=scratch,
        compiler_params=pltpu.CompilerParams(
            dimension_semantics=("arbitrary", "arbitrary"),
            vmem_limit_bytes=vmem_bytes),
        name=f"{kind}_layer{layer}_{'sample' if pos0 else 'prompt'}",
    )(*[a for a, _ in operands])


def _trunk(x, p_all, conv_hist, pool_hist, pos0, weights, *, ns, tl):
    depth = p_all.shape[0]
    h = x
    conv_new, pool_new = [], []
    for i in range(depth):
        kind, hist, new = (("conv", conv_hist, conv_new) if i % 2 == 0
                           else ("pool", pool_hist, pool_new))
        h, st = _layer_call(kind, i, h, p_all, hist, weights,
                            ns=ns, tl=tl, pos0=pos0, final=(i == depth - 1))
        new.append(st)
    return h, jnp.stack(conv_new), jnp.stack(pool_new)


def kernel(x_prompt, x_sample, state_conv, state_pool, p_prompt, p_sample, norm_g, w_in_a, conv_w_a, w_out_a, w_in_b, w_grp_b, scale_b, w_out_b, w_pe, w_pg, final_g):
    n_a, n_b = state_conv.shape[0], state_pool.shape[0]
    bp, seq, _ = x_prompt.shape
    bs, dec_seq, _ = x_sample.shape

    weights = (norm_g[:, None, :],
               (_gate_slabs(w_in_a), conv_w_a, w_out_a.astype(_BF16)),
               (w_in_b.astype(_BF16), w_grp_b.astype(_BF16), scale_b[:, None, :],
                w_out_b.astype(_BF16)),
               w_pe.astype(_BF16), w_pg.astype(_BF16), final_g[None, None, :])

    conv0 = jnp.zeros((n_a, bp, CONV_PAD, E_BRANCH), _F32)
    pool0 = jnp.zeros((n_b, bp, POOL_PAD, E_BRANCH), _F32)
    conv_s = jnp.pad(state_conv, ((0, 0), (0, 0), (CONV_PAD - (CONV_W - 1), 0), (0, 0)))
    pool_s = jnp.pad(state_pool, ((0, 0), (0, 0), (POOL_PAD - POOL_HIST, 0), (0, 0)))

    y_p, conv_p, pool_p = _trunk(x_prompt, p_prompt, conv0, pool0, 0, weights,
                                 ns=1, tl=min(TILE_ROWS, seq))
    y_s, conv_sn, pool_sn = _trunk(x_sample, p_sample, conv_s, pool_s, PAST_LEN, weights,
                                   ns=bs, tl=dec_seq)
    return (y_p, y_s, conv_p, pool_p, conv_sn, pool_sn)
```

```python
import functools

import jax
import jax.numpy as jnp
from jax import lax
from jax.experimental import pallas as pl
from jax.experimental.pallas import tpu as pltpu

D_MODEL = 1024
E_BRANCH = 2048
P_DIM = 256
CONV_W = 3
N_CONV_GATES = 4
POOL_WINDOWS = (2, 4, 8, 16)
G_POOL = E_BRANCH // len(POOL_WINDOWS)
POOL_HIST = max(POOL_WINDOWS) - 1
PAST_LEN = 2048
EPS = 1e-6

SUBLANES = 8
LANES = 128
CONV_PAD = SUBLANES
POOL_PAD = 2 * SUBLANES
SLAB = 512
TILE_ROWS = 2048
SUB_ROWS = 256
Y_SLOTS = 2
V7X_VMEM_BYTES = 64 * 1024 * 1024
VMEM_SPILL_RESERVE = 3 * 1024 * 1024

_BF16 = jnp.bfloat16
_F32 = jnp.float32


def _dot(a, b):
    return jnp.dot(a, b, preferred_element_type=_F32)


def _rmsnorm(x, g):
    r = lax.rsqrt(jnp.mean(x * x, axis=-1, keepdims=True) + EPS)
    return (x * r) * g


def _silu(z):
    return z * jax.nn.sigmoid(z)


def _shift_rows(a, k):
    return pltpu.roll(a, k, 0)


def _epilogue(x, y, p, w_out_refs, w_pe_refs, w_pg_refs, fg_ref, final):
    n_slab = D_MODEL // SLAB
    h = jnp.concatenate([x[:, i * SLAB:(i + 1) * SLAB] + _dot(y, w_out_refs[i][...])
                         for i in range(n_slab)], axis=1)
    h_bf = h.astype(_BF16)
    p_bf = p.astype(_BF16)
    h = jnp.concatenate([h[:, i * SLAB:(i + 1) * SLAB]
                         + _dot(p_bf, w_pe_refs[i][...]) * jax.nn.sigmoid(_dot(h_bf, w_pg_refs[i][...]))
                         for i in range(n_slab)], axis=1)
    if final:
        h = _rmsnorm(h, fg_ref[...])
    return h


def _sub_rows(ns, tl):
    return ns * tl if ns > 1 else min(tl, SUB_ROWS)


def _sub_tiles(x_ref, p_ref, ns, tl):
    if ns > 1:
        yield (slice(0, ns * tl), x_ref[...].reshape(ns * tl, D_MODEL),
               p_ref[...].reshape(ns * tl, P_DIM), tl, 0)
        return
    rows = _sub_rows(ns, tl)
    for r in range(tl // rows):
        rs = slice(r * rows, (r + 1) * rows)
        yield rs, x_ref[0, rs, :], p_ref[0, rs, :], rows, r % Y_SLOTS


def _split_refs(refs, counts):
    out, at = [], 0
    for c in counts:
        if c is None:
            out.append(refs[at])
            at += 1
        else:
            out.append(refs[at:at + c])
            at += c
    assert at == len(refs), (at, len(refs))
    return out


_N_OUT_SLABS = D_MODEL // SLAB
_TAIL_COUNTS = (_N_OUT_SLABS, _N_OUT_SLABS, _N_OUT_SLABS, None, None, None, None, None)


def _conv_layer_kernel(*refs, ns, tl, final):
    (x_ref, p_ref, hist_ref, g_ref, w_in_ref, cw_ref, w_out_refs, w_pe_refs, w_pg_refs, fg_ref,
     o_ref, st_ref, carry, ybuf) = _split_refs(refs, (None,) * 6 + _TAIL_COUNTS)
    @pl.when(pl.program_id(1) == 0)
    def _():
        carry[...] = hist_ref[...]

    n_blk = E_BRANCH // LANES
    hist = [carry[:, :, i * LANES:(i + 1) * LANES] for i in range(n_blk)]
    for rs, x, p, tls, slot in _sub_tiles(x_ref, p_ref, ns, tl):
        hn = _rmsnorm(x, g_ref[...]).astype(_BF16)
        for i in range(n_blk):
            sl = slice(i * LANES, (i + 1) * LANES)
            gates = _dot(hn, w_in_ref[i])
            b_gate, c_gate, hh, zz = (gates[:, q * LANES:(q + 1) * LANES]
                                      for q in range(N_CONV_GATES))
            v = c_gate * hh
            cw = cw_ref[:, sl]
            convs, tails = [], []
            for s in range(ns):
                vp = jnp.concatenate([hist[i][s], v[s * tls:(s + 1) * tls]], axis=0)
                y = cw[CONV_W - 1:CONV_W] * vp
                for k in range(1, CONV_W):
                    y = y + cw[CONV_W - 1 - k:CONV_W - k] * _shift_rows(vp, k)
                convs.append(y[CONV_PAD:])
                tails.append(vp[tls:])
            hist[i] = jnp.stack(tails, axis=0)
            conv = convs[0] if ns == 1 else jnp.concatenate(convs, axis=0)
            ybuf[slot, :, sl] = (b_gate * conv * _silu(zz)).astype(_BF16)
        h = _epilogue(x, ybuf[slot], p, w_out_refs, w_pe_refs, w_pg_refs, fg_ref, final)
        if ns == 1:
            o_ref[0, rs, :] = h
        else:
            o_ref[...] = h.reshape(ns, tl, D_MODEL)

    for i in range(n_blk):
        carry[:, :, i * LANES:(i + 1) * LANES] = hist[i]
    st_ref[...] = carry[:, CONV_PAD - (CONV_W - 1):, :]


def _pool_layer_kernel(*refs, ns, tl, pos0, final):
    n_in_slabs = 2 * E_BRANCH // SLAB
    (x_ref, p_ref, hist_ref, g_ref, w_in_refs, w_grp_ref, scale_ref, w_out_refs, w_pe_refs,
     w_pg_refs, fg_ref, o_ref, st_ref, carry, ybuf) = _split_refs(
         refs, (None,) * 4 + (n_in_slabs, None, None) + _TAIL_COUNTS)
    t = pl.program_id(1)

    @pl.when(t == 0)
    def _():
        carry[...] = hist_ref[...]

    n_grp = len(POOL_WINDOWS)
    hist = [carry[:, :, j * G_POOL:(j + 1) * G_POOL] for j in range(n_grp)]
    for rs, x, p, tls, slot in _sub_tiles(x_ref, p_ref, ns, tl):
        hn = _rmsnorm(x, g_ref[...]).astype(_BF16)
        row0 = 0 if ns > 1 else rs.start
        pos = pos0 + t * tl + row0 + lax.broadcasted_iota(jnp.int32, (tls, LANES), 0)
        for j, w in reversed(list(enumerate(POOL_WINDOWS))):
            sl = slice(j * G_POOL, (j + 1) * G_POOL)
            u = _dot(hn, w_in_refs[j][...])
            zz = _dot(hn, w_in_refs[n_grp + j][...])
            inv_cnt = 1.0 / jnp.minimum(pos + 1, w).astype(_F32)
            inv_cnt = jnp.concatenate([inv_cnt] * (G_POOL // LANES), axis=1)
            ds, tails = [], []
            for s in range(ns):
                us = u[s * tls:(s + 1) * tls]
                up = jnp.concatenate([hist[j][s], us], axis=0)
                acc = up
                k = 1
                while k < w:
                    acc = acc + _shift_rows(acc, k)
                    k *= 2
                ds.append(acc[POOL_PAD:] * inv_cnt - us)
                tails.append(up[tls:])
            hist[j] = jnp.stack(tails, axis=0)
            d = ds[0] if ns == 1 else jnp.concatenate(ds, axis=0)
            mixed = _dot(d.astype(_BF16), w_grp_ref[j]) * scale_ref[:, sl]
            ybuf[slot, :, sl] = (mixed * _silu(zz)).astype(_BF16)
        h = _epilogue(x, ybuf[slot], p, w_out_refs, w_pe_refs, w_pg_refs, fg_ref, final)
        if ns == 1:
            o_ref[0, rs, :] = h
        else:
            o_ref[...] = h.reshape(ns, tl, D_MODEL)

    for j in range(n_grp):
        carry[:, :, j * G_POOL:(j + 1) * G_POOL] = hist[j]
    st_ref[...] = carry[:, POOL_PAD - POOL_HIST:, :]


def _gate_slab_kernel(w_ref, o_ref):
    for i in range(E_BRANCH // LANES):
        for q in range(N_CONV_GATES):
            src = q * E_BRANCH + i * LANES
            o_ref[i, :, q * LANES:(q + 1) * LANES] = w_ref[:, src:src + LANES].astype(_BF16)


def _nbytes(shape, dtype):
    n = jnp.dtype(dtype).itemsize
    for d in shape:
        n *= d
    return n


def _gate_slabs(w):
    n_l, k, n = w.shape
    rows = min(k, SUB_ROWS)
    n_blk = E_BRANCH // LANES
    vmem_bytes = 2 * (_nbytes((rows, n), _F32) + _nbytes((rows, n), _BF16)) + VMEM_SPILL_RESERVE
    return pl.pallas_call(
        _gate_slab_kernel,
        grid=(n_l, k // rows),
        in_specs=[pl.BlockSpec((None, rows, n), lambda a, r: (a, r, 0))],
        out_specs=pl.BlockSpec((None, n_blk, rows, n // n_blk), lambda a, r: (a, 0, r, 0)),
        out_shape=jax.ShapeDtypeStruct((n_l, n_blk, k, n // n_blk), _BF16),
        compiler_params=pltpu.CompilerParams(
            dimension_semantics=("arbitrary", "arbitrary"), vmem_limit_bytes=vmem_bytes),
        name="conv_gate_slabs",
    )(w)


def _whole(arr, idx):
    zeros = (0,) * (arr.ndim - 1)
    return [(arr, pl.BlockSpec((None,) + arr.shape[1:], lambda b, t: (idx,) + zeros,
                               pipeline_mode=pl.Buffered(1)))]


def _col_slabs(arr, idx):
    _, k, n = arr.shape
    return [(arr, pl.BlockSpec((None, k, SLAB), lambda b, t, c=c: (idx, 0, c),
                               pipeline_mode=pl.Buffered(1)))
            for c in range(n // SLAB)]


def _layer_call(kind, layer, x, p_all, hist_all, weights, *, ns, tl, pos0, final):
    (norm_g, (w_in_a, conv_w_a, w_out_a), (w_in_b, w_grp_b, scale_b, w_out_b),
     w_pe, w_pg, final_g) = weights
    n, seq, _ = x.shape
    grid = (n // ns, seq // tl)
    k = layer // 2
    pad = hist_all.shape[2]

    if kind == "conv":
        n_state = CONV_W - 1
        body = functools.partial(_conv_layer_kernel, ns=ns, tl=tl, final=final)
        mixer = _whole(w_in_a, k) + _whole(conv_w_a, k)
        w_out = w_out_a
    else:
        n_state = POOL_HIST
        body = functools.partial(_pool_layer_kernel, ns=ns, tl=tl, pos0=pos0, final=final)
        mixer = _col_slabs(w_in_b, k) + _whole(w_grp_b, k) + _whole(scale_b, k)
        w_out = w_out_b
    params = (_whole(norm_g, layer) + mixer + _col_slabs(w_out, k) + _col_slabs(w_pe, layer)
              + _col_slabs(w_pg, layer) + _whole(final_g, 0))

    tiles = [
        (x, pl.BlockSpec((ns, tl, D_MODEL), lambda b, t: (b, t, 0))),
        (p_all, pl.BlockSpec((None, ns, tl, P_DIM), lambda b, t: (layer, b, t, 0))),
        (hist_all, pl.BlockSpec((None, ns, pad, E_BRANCH), lambda b, t: (k, b, 0, 0))),
    ]
    out_shape = (jax.ShapeDtypeStruct((n, seq, D_MODEL), _F32),
                 jax.ShapeDtypeStruct((n, n_state, E_BRANCH), _F32))
    out_specs = (pl.BlockSpec((ns, tl, D_MODEL), lambda b, t: (b, t, 0)),
                 pl.BlockSpec((ns, n_state, E_BRANCH), lambda b, t: (b, 0, 0)))
    y_shape = (Y_SLOTS, _sub_rows(ns, tl), E_BRANCH)
    scratch = [pltpu.VMEM((ns, pad, E_BRANCH), _F32), pltpu.VMEM(y_shape, _BF16)]

    def block_bytes(arr, spec):
        return _nbytes([d for d in spec.block_shape if d is not None], arr.dtype)

    vmem_bytes = (sum(block_bytes(a, s) for a, s in params)
                  + 2 * sum(block_bytes(a, s) for a, s in tiles)
                  + 2 * sum(block_bytes(a, s) for a, s in zip(out_shape, out_specs))
                  + _nbytes((ns, pad, E_BRANCH), _F32) + _nbytes(y_shape, _BF16)
                  + VMEM_SPILL_RESERVE)
    assert vmem_bytes <= V7X_VMEM_BYTES, vmem_bytes

    operands = tiles + params
    return pl.pallas_call(
        body,
        grid=grid,
        in_specs=[s for _, s in operands],
        out_specs=out_specs,
        out_shape=out_shape,
        scratch_shapes=scratch,
        compiler_params=pltpu.CompilerParams(
            dimension_semantics=("arbitrary", "arbitrary"),
            vmem_limit_bytes=vmem_bytes),
        name=f"{kind}_layer{layer}_{'sample' if pos0 else 'prompt'}",
    )(*[a for a, _ in operands])


def _trunk(x, p_all, conv_hist, pool_hist, pos0, weights, *, ns, tl):
    depth = p_all.shape[0]
    h = x
    conv_new, pool_new = [], []
    for i in range(depth):
        kind, hist, new = (("conv", conv_hist, conv_new) if i % 2 == 0
                           else ("pool", pool_hist, pool_new))
        h, st = _layer_call(kind, i, h, p_all, hist, weights,
                            ns=ns, tl=tl, pos0=pos0, final=(i == depth - 1))
        new.append(st)
    return h, jnp.stack(conv_new), jnp.stack(pool_new)


def kernel(x_prompt, x_sample, state_conv, state_pool, p_prompt, p_sample, norm_g, w_in_a, conv_w_a, w_out_a, w_in_b, w_grp_b, scale_b, w_out_b, w_pe, w_pg, final_g):
    n_a, n_b = state_conv.shape[0], state_pool.shape[0]
    bp, seq, _ = x_prompt.shape
    bs, dec_seq, _ = x_sample.shape

    weights = (norm_g[:, None, :],
               (_gate_slabs(w_in_a), conv_w_a, w_out_a.astype(_BF16)),
               (w_in_b.astype(_BF16), w_grp_b.astype(_BF16), scale_b[:, None, :],
                w_out_b.astype(_BF16)),
               w_pe.astype(_BF16), w_pg.astype(_BF16), final_g[None, None, :])

    conv0 = jnp.zeros((n_a, bp, CONV_PAD, E_BRANCH), _F32)
    pool0 = jnp.zeros((n_b, bp, POOL_PAD, E_BRANCH), _F32)
    conv_s = jnp.pad(state_conv, ((0, 0), (0, 0), (CONV_PAD - (CONV_W - 1), 0), (0, 0)))
    pool_s = jnp.pad(state_pool, ((0, 0), (0, 0), (POOL_PAD - POOL_HIST, 0), (0, 0)))

    y_p, conv_p, pool_p = _trunk(x_prompt, p_prompt, conv0, pool0, 0, weights,
                                 ns=1, tl=min(TILE_ROWS, seq))
    y_s, conv_sn, pool_sn = _trunk(x_sample, p_sample, conv_s, pool_s, PAST_LEN, weights,
                                   ns=bs, tl=dec_seq)
    return (y_p, y_s, conv_p, pool_p, conv_sn, pool_sn)
```
